```python
import jax, jax.numpy as jnp
from jax import lax
import numpy as np

D_MODEL = 1024
BATCH = 16
SEQ = 2048
DEPTH = 2

GRID_W = 64
CTX_LEN = 256
EPS = 1e-6
N_MOD = 6

HEAD_DIM = 128
N_HEADS = 8
N_KV_HEADS = 2
GQA_GROUP = N_HEADS // N_KV_HEADS
Q_BLOCK = 128
ROPE_THETA = 10000.0
CONV_WIDTH = 512
CONV_KERNEL = 31
FNET_GROUPS = 4
FNET_GROUP_DIM = 128
FNET_WIDTH = FNET_GROUPS * FNET_GROUP_DIM
N_BRANCHES = 3
N_EXPERTS = 32
TOP_K = 4
D_FF = D_MODEL
SWIGLU_LIMIT = 7.0
SWIGLU_ALPHA = 1.702

Q_DIM = N_HEADS * HEAD_DIM
KV_DIM = N_KV_HEADS * HEAD_DIM
OFF_Q = 0
OFF_K = OFF_Q + Q_DIM
OFF_V = OFF_K + KV_DIM
OFF_CONV = OFF_V + KV_DIM
OFF_FNET = OFF_CONV + 2 * CONV_WIDTH
OFF_GATE = OFF_FNET + FNET_WIDTH
IN_COLS = OFF_GATE + N_BRANCHES * D_MODEL

kernel_name = "hybrid_gated_attn_conv_fnet_moe_dit"


def _rms(x, g):
    xf = x.astype(jnp.float32)
    y = xf * lax.rsqrt(jnp.mean(xf * xf, axis=-1, keepdims=True) + EPS)
    return (y * g.astype(jnp.float32)).astype(x.dtype)


def _layernorm(x, g, b):
    xf = x.astype(jnp.float32)
    mu = jnp.mean(xf, axis=-1, keepdims=True)
    xc = xf - mu
    var = jnp.mean(xc * xc, axis=-1, keepdims=True)
    y = xc * lax.rsqrt(var + EPS) * g.astype(jnp.float32) + b.astype(jnp.float32)
    return y.astype(x.dtype)


def _modulate(h, shift, scale):
    return h * (1 + scale[:, None, :]) + shift[:, None, :]


def _rope_tables(rows, dtype):
    row = jnp.repeat(jnp.arange(rows, dtype=jnp.float32), GRID_W)
    col = jnp.tile(jnp.arange(GRID_W, dtype=jnp.float32), rows)
    n_freq = HEAD_DIM // 4
    inv_freq = ROPE_THETA ** (-jnp.arange(n_freq, dtype=jnp.float32) / n_freq)
    ang_r = row[:, None] * inv_freq[None, :]
    ang_c = col[:, None] * inv_freq[None, :]
    ang = jnp.concatenate([ang_r, ang_r, ang_c, ang_c], axis=-1)
    return jnp.cos(ang).astype(dtype), jnp.sin(ang).astype(dtype)


def _rot_half_axial(t):
    ts = t.reshape(t.shape[:-1] + (2, 2, HEAD_DIM // 4))
    ts = jnp.stack([-ts[..., 1, :], ts[..., 0, :]], axis=-2)
    return ts.reshape(t.shape)


def _rope(t, cos, sin):
    return t * cos[:, None, :] + _rot_half_axial(t) * sin[:, None, :]


def _kv_heads(ukv, kn_g):
    b, l, _ = ukv.shape
    k = _rms(ukv[..., :KV_DIM].reshape(b, l, N_KV_HEADS, HEAD_DIM), kn_g)
    v = ukv[..., KV_DIM:].reshape(b, l, N_KV_HEADS, HEAD_DIM)
    return k, v


def _qkv_heads(u, qn_g, kn_g):
    b, l, _ = u.shape
    q = _rms(u[..., OFF_Q:OFF_K].reshape(b, l, N_HEADS, HEAD_DIM), qn_g)
    k, v = _kv_heads(u[..., OFF_K:OFF_CONV], kn_g)
    return q, k, v


def _attend(q, k, v, q_block):
    b, lq = q.shape[:2]
    nblk = lq // q_block
    qb = jnp.moveaxis(q.reshape(b, nblk, q_block, N_KV_HEADS, GQA_GROUP, HEAD_DIM), 1, 0)
    scale = HEAD_DIM ** -0.5

    def one_block(qi):
        s = jnp.einsum('bqkgd,bskd->bkgqs', qi, k, preferred_element_type=jnp.float32) * scale
        p = jax.nn.softmax(s, axis=-1).astype(v.dtype)
        return jnp.einsum('bkgqs,bskd->bqkgd', p, v)

    o = lax.map(one_block, qb)
    return jnp.moveaxis(o, 0, 1).reshape(b, lq, Q_DIM)


def _dwconv(h, w, bias):
    pad = CONV_KERNEL // 2
    y = lax.conv_general_dilated(h, w[:, None, :].astype(h.dtype), window_strides=(1,),
                                 padding=[(pad, pad)], dimension_numbers=('NWC', 'WIO', 'NWC'),
                                 feature_group_count=h.shape[-1])
    return y + bias


def _conv_branch(u, dw_w, dw_b, ln_g, ln_b, w_o):
    a, g = jnp.split(u, 2, axis=-1)
    h = a * jax.nn.sigmoid(g)
    h = _dwconv(h, dw_w, dw_b)
    h = jax.nn.silu(_layernorm(h, ln_g, ln_b))
    return h @ w_o


def _fnet_branch(u, w_o):
    b, l, _ = u.shape
    ug = u.reshape(b, l, FNET_GROUPS, FNET_GROUP_DIM).astype(jnp.float32)
    f = jnp.fft.fft2(ug, axes=(1, 3), norm='ortho').real.astype(u.dtype)
    return f.reshape(b, l, FNET_WIDTH) @ w_o


def _token_mixer(u, q, k, v, q_block, w_attn_o, conv_dw_w, conv_dw_b, conv_ln_g, conv_ln_b,
                 w_conv_o, w_fnet_o, w_out):
    b, l, _ = u.shape
    y_attn = _attend(q, k, v, q_block) @ w_attn_o
    y_conv = _conv_branch(u[..., OFF_CONV:OFF_FNET], conv_dw_w, conv_dw_b, conv_ln_g, conv_ln_b, w_conv_o)
    y_fnet = _fnet_branch(u[..., OFF_FNET:OFF_GATE], w_fnet_o)
    gates = jax.nn.sigmoid(u[..., OFF_GATE:IN_COLS].reshape(b, l, N_BRANCHES, D_MODEL))
    merged = gates[..., 0, :] * y_attn + gates[..., 1, :] * y_conv + gates[..., 2, :] * y_fnet
    return merged @ w_out


def _moe(h, router_w, router_b, w_gate, b_gate, w_up, b_up, w_down, b_down):
    b, l, d = h.shape
    t = h.reshape(b * l, d)
    logits = (t @ router_w + router_b).astype(jnp.float32)
    top_v, top_i = lax.top_k(logits, TOP_K)
    probs = jax.nn.softmax(top_v, axis=-1)
    combine = jnp.sum(jax.nn.one_hot(top_i, N_EXPERTS, dtype=jnp.float32) * probs[..., None], axis=1)
    out = jnp.zeros(t.shape, jnp.float32)
    for e in range(N_EXPERTS):
        g = jnp.minimum(t @ w_gate[e] + b_gate[e], SWIGLU_LIMIT)
        up = jnp.clip(t @ w_up[e] + b_up[e], -SWIGLU_LIMIT, SWIGLU_LIMIT)
        y = ((up + 1) * (g * jax.nn.sigmoid(SWIGLU_ALPHA * g))) @ w_down[e] + b_down[e]
        out = out + combine[:, e:e + 1] * y
    return out.astype(h.dtype).reshape(b, l, d)


def setup_inputs(seed: int = 0) -> dict:
    key = jax.random.key(seed)
    ks = jax.random.split(key, 29)

    def nrm(k, shape, s):
        return jax.random.normal(k, shape, jnp.float32) * s

    L = DEPTH
    return {
        "x": nrm(ks[0], (BATCH, SEQ, D_MODEL), 1.0),
        "c": nrm(ks[1], (BATCH, D_MODEL), 1.0),
        "ctx": nrm(ks[2], (BATCH, CTX_LEN, D_MODEL), 1.0),
        "c_ctx": nrm(ks[3], (D_MODEL,), 1.0),
        "w_ada": nrm(ks[4], (L, D_MODEL, N_MOD * D_MODEL), 0.5 * D_MODEL ** -0.5),
        "b_ada": nrm(ks[5], (L, N_MOD * D_MODEL), 0.02),
        "g_pre_mix": 1.0 + nrm(ks[6], (L, D_MODEL), 0.02),
        "g_post_mix": 1.0 + nrm(ks[7], (L, D_MODEL), 0.02),
        "g_pre_ffn": 1.0 + nrm(ks[8], (L, D_MODEL), 0.02),
        "g_post_ffn": 1.0 + nrm(ks[9], (L, D_MODEL), 0.02),
        "w_in": nrm(ks[10], (L, D_MODEL, IN_COLS), D_MODEL ** -0.5),
        "q_norm_g": 1.0 + nrm(ks[11], (L, HEAD_DIM), 0.02),
        "k_norm_g": 1.0 + nrm(ks[12], (L, HEAD_DIM), 0.02),
        "w_attn_o": nrm(ks[13], (L, Q_DIM, D_MODEL), Q_DIM ** -0.5),
        "conv_dw_w": nrm(ks[14], (L, CONV_KERNEL, CONV_WIDTH), CONV_KERNEL ** -0.5),
        "conv_dw_b": nrm(ks[15], (L, CONV_WIDTH), 0.02),
        "conv_ln_g": 1.0 + nrm(ks[16], (L, CONV_WIDTH), 0.02),
        "conv_ln_b": nrm(ks[17], (L, CONV_WIDTH), 0.02),
        "w_conv_o": nrm(ks[18], (L, CONV_WIDTH, D_MODEL), CONV_WIDTH ** -0.5),
        "w_fnet_o": nrm(ks[19], (L, FNET_WIDTH, D_MODEL), FNET_WIDTH ** -0.5),
        "w_out": nrm(ks[20], (L, D_MODEL, D_MODEL), D_MODEL ** -0.5),
        "router_w": nrm(ks[21], (L, D_MODEL, N_EXPERTS), D_MODEL ** -0.5),
        "router_b": nrm(ks[22], (L, N_EXPERTS), 0.01),
        "w_gate": nrm(ks[23], (L, N_EXPERTS, D_MODEL, D_FF), D_MODEL ** -0.5),
        "b_gate": nrm(ks[24], (L, N_EXPERTS, D_FF), 0.01),
        "w_up": nrm(ks[25], (L, N_EXPERTS, D_MODEL, D_FF), D_MODEL ** -0.5),
        "b_up": nrm(ks[26], (L, N_EXPERTS, D_FF), 0.01),
        "w_down": nrm(ks[27], (L, N_EXPERTS, D_FF, D_MODEL), D_FF ** -0.5),
        "b_down": nrm(ks[28], (L, N_EXPERTS, D_MODEL), 0.01),
    }


def reference(x, c, ctx, c_ctx, w_ada, b_ada, g_pre_mix, g_post_mix, g_pre_ffn, g_post_ffn,
              w_in, q_norm_g, k_norm_g, w_attn_o, conv_dw_w, conv_dw_b, conv_ln_g, conv_ln_b,
              w_conv_o, w_fnet_o, w_out, router_w, router_b, w_gate, b_gate, w_up, b_up,
              w_down, b_down):
    rows = x.shape[1] // GRID_W
    cos, sin = _rope_tables(rows, x.dtype)
    silu_c = jax.nn.silu(c)
    silu_cc = jax.nn.silu(c_ctx)[None, :]
    xl, xc = x, ctx
    for l in range(DEPTH):
        last = l == DEPTH - 1
        mod_l = (silu_c @ w_ada[l] + b_ada[l]).reshape(-1, N_MOD, D_MODEL)
        mod_c = (silu_cc @ w_ada[l] + b_ada[l]).reshape(1, N_MOD, D_MODEL)
        mix_p = (w_attn_o[l], conv_dw_w[l], conv_dw_b[l], conv_ln_g[l], conv_ln_b[l],
                 w_conv_o[l], w_fnet_o[l], w_out[l])
        moe_p = (router_w[l], router_b[l], w_gate[l], b_gate[l], w_up[l], b_up[l],
                 w_down[l], b_down[l])

        hl = _modulate(_rms(xl, g_pre_mix[l]), mod_l[:, 0], mod_l[:, 1])
        hc = _modulate(_rms(xc, g_pre_mix[l]), mod_c[:, 0], mod_c[:, 1])
        if last:
            kc, vc = _kv_heads(hc @ w_in[l][:, OFF_K:OFF_CONV], k_norm_g[l])
        else:
            uc = hc @ w_in[l]
            qc, kc, vc = _qkv_heads(uc, q_norm_g[l], k_norm_g[l])
        ul = hl @ w_in[l]
        ql, kl, vl = _qkv_heads(ul, q_norm_g[l], k_norm_g[l])
        ql, kl = _rope(ql, cos, sin), _rope(kl, cos, sin)
        k_all = jnp.concatenate([kl, kc], axis=1)
        v_all = jnp.concatenate([vl, vc], axis=1)
        mix_l = _token_mixer(ul, ql, k_all, v_all, Q_BLOCK, *mix_p)
        xl_new = xl + mod_l[:, 2][:, None, :] * _rms(mix_l, g_post_mix[l])
        if not last:
            mix_c = _token_mixer(uc, qc, kc, vc, CTX_LEN, *mix_p)
            xc = xc + mod_c[:, 2][:, None, :] * _rms(mix_c, g_post_mix[l])
        xl = xl_new

        hl = _modulate(_rms(xl, g_pre_ffn[l]), mod_l[:, 3], mod_l[:, 4])
        if last:
            yl = _moe(hl, *moe_p)
        else:
            hc = _modulate(_rms(xc, g_pre_ffn[l]), mod_c[:, 3], mod_c[:, 4])
            y = _moe(jnp.concatenate([hc, hl], axis=1), *moe_p)
            yc, yl = y[:, :xc.shape[1]], y[:, xc.shape[1]:]
            xc = xc + mod_c[:, 5][:, None, :] * _rms(yc, g_post_ffn[l])
        xl = xl + mod_l[:, 5][:, None, :] * _rms(yl, g_post_ffn[l])
    return xl
```

```python
import functools

import jax
import jax.numpy as jnp
import numpy as np
from jax import lax
from jax.experimental import pallas as pl
from jax.experimental.pallas import tpu as pltpu

F32 = jnp.float32
BF16 = jnp.bfloat16
HIGHEST = lax.Precision.HIGHEST

D_MODEL = 1024
GRID_W = 64
EPS = 1e-6
N_MOD = 6
HEAD_DIM = 128
N_HEADS = 8
N_KV_HEADS = 2
GQA_GROUP = N_HEADS // N_KV_HEADS
ROPE_THETA = 10000.0
CONV_WIDTH = 512
CONV_KERNEL = 31
CONV_PAD = 16
FNET_GROUPS = 4
FNET_GROUP_DIM = 128
FNET_WIDTH = FNET_GROUPS * FNET_GROUP_DIM
N_EXPERTS = 32
TOP_K = 4
SWIGLU_LIMIT = 7.0
SWIGLU_ALPHA = 1.702

Q_DIM = N_HEADS * HEAD_DIM
KV_DIM = N_KV_HEADS * HEAD_DIM
OFF_K = Q_DIM
OFF_V = OFF_K + KV_DIM
OFF_CONV = OFF_V + KV_DIM
OFF_FNET = OFF_CONV + 2 * CONV_WIDTH
OFF_GATE = OFF_FNET + FNET_WIDTH
IN_COLS = OFF_GATE + 3 * D_MODEL

LANES = 128
TOKEN_TILE = 256
MOE_TILE = 512
MOD_ROWS = 24
CONV_ROWS = 32
NEG_BIG = -1e30
VMEM_LIMIT = 56 * 1024 * 1024


def _params(sem, vmem=None):
    return pltpu.CompilerParams(dimension_semantics=sem, vmem_limit_bytes=vmem)


def _resident(shape, index_map):
    return pl.BlockSpec(shape, index_map, pipeline_mode=pl.Buffered(1))


def _rms(x, g):
    return x * lax.rsqrt(jnp.mean(x * x, axis=-1, keepdims=True) + EPS) * g


def _sigmoid(x):
    return 1.0 / (1.0 + jnp.exp(-x))


def _dot(a, b):
    return jnp.dot(a, b, preferred_element_type=F32)


def _adaln_kernel(c_ref, w_ref, b_ref, o_ref):
    c = c_ref[...]
    s = c * _sigmoid(c)
    o_ref[0] = jnp.dot(s, w_ref[0], precision=HIGHEST, preferred_element_type=F32) + b_ref[0]


def _adaln(c_rows, w_ada, b_ada):
    depth, _, n = w_ada.shape
    tn = 1024
    return pl.pallas_call(
        _adaln_kernel,
        grid=(depth, n // tn),
        in_specs=[
            pl.BlockSpec((MOD_ROWS, D_MODEL), lambda l, j: (0, 0)),
            pl.BlockSpec((1, D_MODEL, tn), lambda l, j: (l, 0, j)),
            pl.BlockSpec((1, 1, tn), lambda l, j: (l, 0, j)),
        ],
        out_specs=pl.BlockSpec((1, MOD_ROWS, tn), lambda l, j: (l, 0, j)),
        out_shape=jax.ShapeDtypeStruct((depth, MOD_ROWS, n), F32),
        compiler_params=_params(("arbitrary", "arbitrary")),
        name="adaln",
    )(c_rows, w_ada, b_ada.reshape(depth, 1, n))


def _mod_spec(j, n_lat_tiles, ctx_row):
    def index_map(b, t):
        row = jnp.where(t >= n_lat_tiles, ctx_row, b)
        return (row * N_MOD + j, 0, 0)
    return pl.BlockSpec((1, 1, D_MODEL), index_map)


def _inproj_kernel(x_ref, shift_ref, scale_ref, g_ref, w_ref, o_ref):
    h = _rms(x_ref[0], g_ref[...]) * (1.0 + scale_ref[0]) + shift_ref[0]
    hb = h.astype(BF16)
    n = w_ref.shape[1]
    nc = min(n, 1024)
    for j in range(n // nc):
        o_ref[0, :, j * nc:(j + 1) * nc] = _dot(hb, w_ref[:, j * nc:(j + 1) * nc]).astype(BF16)


def _inproj(x_all, mod, g_pre, w_bf, *, batch, first_tile, n_tiles, n_lat_tiles, out_rows):
    n = w_bf.shape[1]
    return pl.pallas_call(
        _inproj_kernel,
        grid=(batch, n_tiles),
        in_specs=[
            pl.BlockSpec((1, TOKEN_TILE, D_MODEL), lambda b, t: (b, t + first_tile, 0)),
            _mod_spec(0, n_lat_tiles - first_tile, batch),
            _mod_spec(1, n_lat_tiles - first_tile, batch),
            pl.BlockSpec((1, D_MODEL), lambda b, t: (0, 0)),
            _resident((D_MODEL, n), lambda b, t: (0, 0)),
        ],
        out_specs=pl.BlockSpec((1, TOKEN_TILE, n), lambda b, t: (b, t, 0)),
        out_shape=jax.ShapeDtypeStruct((batch, out_rows, n), BF16),
        compiler_params=_params(("arbitrary", "arbitrary"), VMEM_LIMIT),
        name="inproj",
    )(x_all, mod, mod, g_pre.reshape(1, D_MODEL), w_bf)


def _rope(t, cos, sin_signed, first_half):
    rot = jnp.where(first_half, pltpu.roll(t, HEAD_DIM - 32, 1), pltpu.roll(t, 32, 1))
    return t * cos + rot * sin_signed


def _attn_kernel(q_ref, kl_ref, vl_ref, kc_ref, vc_ref, cosq_ref, sinq_ref, cosk_ref, sink_ref,
                 qg_ref, kg_ref, o_ref, kp_ref, vp_ref, *, n_lat, n_ctx, n_lat_tiles, ctx_queries):
    qi = pl.program_id(2)
    lane = lax.broadcasted_iota(jnp.int32, (1, HEAD_DIM), 1)
    first_half = (lane & 32) == 0

    @pl.when(qi == 0)
    def _():
        kl = _rms(kl_ref[0].astype(F32), kg_ref[...])
        kp_ref[0:n_lat, :] = _rope(kl, cosk_ref[0:n_lat, :], sink_ref[0:n_lat, :], first_half).astype(BF16)
        kp_ref[n_lat:n_lat + n_ctx, :] = _rms(kc_ref[0].astype(F32), kg_ref[...]).astype(BF16)
        vp_ref[0:n_lat, :] = vl_ref[0]
        vp_ref[n_lat:n_lat + n_ctx, :] = vc_ref[0]

    scale = HEAD_DIM ** -0.5
    if ctx_queries:
        key_is_latent = lax.broadcasted_iota(jnp.int32, (1, n_lat + n_ctx), 1) < n_lat
        bias = jnp.where(key_is_latent, jnp.where(qi >= n_lat_tiles, NEG_BIG, 0.0), 0.0)
    for g in range(GQA_GROUP):
        q = _rms(q_ref[0, :, g * HEAD_DIM:(g + 1) * HEAD_DIM].astype(F32), qg_ref[...])
        q = _rope(q, cosq_ref[...], sinq_ref[...], first_half) * scale
        s = lax.dot_general(q.astype(BF16), kp_ref[...], (((1,), (1,)), ((), ())),
                            preferred_element_type=F32)
        if ctx_queries:
            s = s + bias
        m = jnp.max(s, axis=-1, keepdims=True)
        p = jnp.exp(s - m)
        l = jnp.sum(p, axis=-1, keepdims=True)
        o = _dot(p.astype(BF16), vp_ref[...])
        o_ref[0, :, g * HEAD_DIM:(g + 1) * HEAD_DIM] = (o / l).astype(BF16)


def _attention(u_all, kv_ctx, kc_blk, cos, sin_signed, q_gain, k_gain, *, batch, n_lat, n_ctx,
               n_q_tiles, out_rows):
    n_lat_tiles = n_lat // TOKEN_TILE
    kcol = OFF_K // HEAD_DIM
    vcol = OFF_V // HEAD_DIM
    kc_row, kc_kcol, kc_vcol = kc_blk
    kernel = functools.partial(_attn_kernel, n_lat=n_lat, n_ctx=n_ctx, n_lat_tiles=n_lat_tiles,
                               ctx_queries=n_q_tiles > n_lat_tiles)
    width = GQA_GROUP * HEAD_DIM
    return pl.pallas_call(
        kernel,
        grid=(batch, N_KV_HEADS, n_q_tiles),
        in_specs=[
            pl.BlockSpec((1, TOKEN_TILE, width), lambda b, h, t: (b, t, h)),
            pl.BlockSpec((1, n_lat, HEAD_DIM), lambda b, h, t: (b, 0, kcol + h)),
            pl.BlockSpec((1, n_lat, HEAD_DIM), lambda b, h, t: (b, 0, vcol + h)),
            pl.BlockSpec((1, n_ctx, HEAD_DIM), lambda b, h, t: (b, kc_row, kc_kcol + h)),
            pl.BlockSpec((1, n_ctx, HEAD_DIM), lambda b, h, t: (b, kc_row, kc_vcol + h)),
            pl.BlockSpec((TOKEN_TILE, HEAD_DIM), lambda b, h, t: (t, 0)),
            pl.BlockSpec((TOKEN_TILE, HEAD_DIM), lambda b, h, t: (t, 0)),
            pl.BlockSpec((n_lat + n_ctx, HEAD_DIM), lambda b, h, t: (0, 0)),
            pl.BlockSpec((n_lat + n_ctx, HEAD_DIM), lambda b, h, t: (0, 0)),
            pl.BlockSpec((1, HEAD_DIM), lambda b, h, t: (0, 0)),
            pl.BlockSpec((1, HEAD_DIM), lambda b, h, t: (0, 0)),
        ],
        out_specs=pl.BlockSpec((1, TOKEN_TILE, width), lambda b, h, t: (b, t, h)),
        out_shape=jax.ShapeDtypeStruct((batch, out_rows, Q_DIM), BF16),
        scratch_shapes=[pltpu.VMEM((n_lat + n_ctx, HEAD_DIM), BF16),
                        pltpu.VMEM((n_lat + n_ctx, HEAD_DIM), BF16)],
        compiler_params=_params(("arbitrary", "arbitrary", "arbitrary"), VMEM_LIMIT),
        name="attention",
    )(u_all, u_all, u_all, kv_ctx, kv_ctx, cos, sin_signed, cos, sin_signed,
      q_gain.reshape(1, HEAD_DIM), k_gain.reshape(1, HEAD_DIM))


def _conv_segment(a_ref, g_ref, w_ref, b_ref, lng_ref, lnb_ref, o_ref, hp_ref, row0, n_rows):
    zeros = jnp.zeros((CONV_PAD, CONV_WIDTH), F32)
    hp_ref[0:CONV_PAD, :] = zeros
    hp_ref[CONV_PAD + n_rows:2 * CONV_PAD + n_rows, :] = zeros
    chunk = 256

    def glu(c, carry):
        src = pl.ds(pl.multiple_of(row0 + c * chunk, chunk), chunk)
        a = a_ref[0, src, :].astype(F32)
        g = g_ref[0, src, :].astype(F32)
        hp_ref[pl.ds(pl.multiple_of(CONV_PAD + c * chunk, CONV_PAD), chunk), :] = a * _sigmoid(g)
        return carry
    lax.fori_loop(0, n_rows // chunk, glu, 0)

    slab_rows = CONV_ROWS + 2 * CONV_PAD
    half = CONV_KERNEL // 2
    shifts = sorted((k - half + CONV_PAD for k in range(CONV_KERNEL)), key=lambda s: (s % 8, s))

    def conv(c, carry):
        r = pl.multiple_of(c * CONV_ROWS, CONV_ROWS)
        slab = hp_ref[pl.ds(r, slab_rows), :]
        acc = jnp.zeros((CONV_ROWS, CONV_WIDTH), F32) + b_ref[...]
        rotated, residue = slab, 0
        for shift in shifts:
            if shift % 8 != residue:
                residue = shift % 8
                rotated = pltpu.roll(slab, slab_rows - residue, 0)
            base = shift - residue
            k = shift + half - CONV_PAD
            acc = acc + rotated[base:base + CONV_ROWS, :] * w_ref[k:k + 1, :]
        mu = jnp.mean(acc, axis=-1, keepdims=True)
        xc = acc - mu
        var = jnp.mean(xc * xc, axis=-1, keepdims=True)
        y = xc * lax.rsqrt(var + EPS) * lng_ref[...] + lnb_ref[...]
        dst = pl.ds(pl.multiple_of(row0 + c * CONV_ROWS, CONV_ROWS), CONV_ROWS)
        o_ref[0, dst, :] = (y * _sigmoid(y)).astype(BF16)
        return carry
    lax.fori_loop(0, n_rows // CONV_ROWS, conv, 0)


def _conv_kernel(a_ref, g_ref, w_ref, b_ref, lng_ref, lnb_ref, o_ref, hp_ref, *, n_lat, n_ctx, with_ctx):
    _conv_segment(a_ref, g_ref, w_ref, b_ref, lng_ref, lnb_ref, o_ref, hp_ref, 0, n_lat)
    if with_ctx:
        _conv_segment(a_ref, g_ref, w_ref, b_ref, lng_ref, lnb_ref, o_ref, hp_ref, n_lat, n_ctx)


def _conv_branch(u_all, dw_w, dw_b, ln_g, ln_b, *, batch, n_lat, n_ctx, with_ctx):
    rows = u_all.shape[1]
    kernel = functools.partial(_conv_kernel, n_lat=n_lat, n_ctx=n_ctx, with_ctx=with_ctx)
    acol = OFF_CONV // CONV_WIDTH
    vec = lambda v: v.reshape(1, CONV_WIDTH)
    small = pl.BlockSpec((1, CONV_WIDTH), lambda b: (0, 0))
    return pl.pallas_call(
        kernel,
        grid=(batch,),
        in_specs=[
            pl.BlockSpec((1, rows, CONV_WIDTH), lambda b: (b, 0, acol)),
            pl.BlockSpec((1, rows, CONV_WIDTH), lambda b: (b, 0, acol + 1)),
            pl.BlockSpec((CONV_KERNEL, CONV_WIDTH), lambda b: (0, 0)),
            small, small, small,
        ],
        out_specs=pl.BlockSpec((1, rows, CONV_WIDTH), lambda b: (b, 0, 0)),
        out_shape=jax.ShapeDtypeStruct((batch, rows, CONV_WIDTH), BF16),
        scratch_shapes=[pltpu.VMEM((n_lat + 2 * CONV_PAD, CONV_WIDTH), F32)],
        compiler_params=_params(("arbitrary",), VMEM_LIMIT),
        name="conv_branch",
    )(u_all, u_all, dw_w, vec(dw_b), vec(ln_g), vec(ln_b))


def _dft_tables(n):
    t = lax.broadcasted_iota(jnp.int32, (n, n), 0)
    k = lax.broadcasted_iota(jnp.int32, (n, n), 1)
    ang = ((t * k) % n).astype(F32) * (2.0 * np.pi / n)
    scale = (n * FNET_GROUP_DIM) ** -0.5
    return jnp.concatenate([jnp.cos(ang) * scale, -jnp.sin(ang) * scale], axis=1).astype(BF16)


def _channel_dft():
    t = lax.broadcasted_iota(jnp.int32, (FNET_GROUP_DIM, FNET_GROUP_DIM), 0)
    k = lax.broadcasted_iota(jnp.int32, (FNET_GROUP_DIM, FNET_GROUP_DIM), 1)
    ang = ((t * k) % FNET_GROUP_DIM).astype(F32) * (2.0 * np.pi / FNET_GROUP_DIM)
    return jnp.concatenate([jnp.cos(ang), jnp.sin(ang)], axis=1).astype(BF16)


def _fnet_kernel(x_ref, cs_ref, wl_ref, wc_ref, o_ref, abl_ref, abc_ref, *, n_lat, n_ctx, with_ctx):
    t = pl.program_id(1)
    n_lat_tiles = n_lat // TOKEN_TILE

    def channel_pass(row0, n, ab_ref):
        for grp in range(FNET_GROUPS):
            cols = slice(grp * FNET_GROUP_DIM, (grp + 1) * FNET_GROUP_DIM)
            y = _dot(x_ref[0, row0:row0 + n, cols], cs_ref[...])
            ab_ref[0:n, cols] = y[:, :FNET_GROUP_DIM].astype(BF16)
            ab_ref[n:2 * n, cols] = y[:, FNET_GROUP_DIM:].astype(BF16)

    @pl.when(t == 0)
    def _():
        channel_pass(0, n_lat, abl_ref)
        if with_ctx:
            channel_pass(n_lat, n_ctx, abc_ref)

    @pl.when(t < n_lat_tiles)
    def _():
        o_ref[0] = _dot(wl_ref[...], abl_ref[...]).astype(BF16)

    if with_ctx:
        @pl.when(t >= n_lat_tiles)
        def _():
            o_ref[0] = _dot(wc_ref[...], abc_ref[...]).astype(BF16)


def _fnet_branch(u_all, cs128, w_lat, w_ctx, *, batch, n_lat, n_ctx, with_ctx):
    rows = u_all.shape[1]
    n_lat_tiles = n_lat // TOKEN_TILE
    n_tiles = n_lat_tiles + (1 if with_ctx else 0)
    kernel = functools.partial(_fnet_kernel, n_lat=n_lat, n_ctx=n_ctx, with_ctx=with_ctx)
    xcol = OFF_FNET // FNET_WIDTH
    return pl.pallas_call(
        kernel,
        grid=(batch, n_tiles),
        in_specs=[
            pl.BlockSpec((1, rows, FNET_WIDTH), lambda b, t: (b, 0, xcol)),
            pl.BlockSpec((FNET_GROUP_DIM, 2 * FNET_GROUP_DIM), lambda b, t: (0, 0)),
            pl.BlockSpec((TOKEN_TILE, 2 * n_lat), lambda b, t: (jnp.minimum(t, n_lat_tiles - 1), 0)),
            pl.BlockSpec((n_ctx, 2 * n_ctx), lambda b, t: (0, 0)),
        ],
        out_specs=pl.BlockSpec((1, TOKEN_TILE, FNET_WIDTH), lambda b, t: (b, t, 0)),
        out_shape=jax.ShapeDtypeStruct((batch, rows, FNET_WIDTH), BF16),
        scratch_shapes=[pltpu.VMEM((2 * n_lat, FNET_WIDTH), BF16),
                        pltpu.VMEM((2 * n_ctx, FNET_WIDTH), BF16)],
        compiler_params=_params(("arbitrary", "arbitrary"), VMEM_LIMIT),
        name="fnet_branch",
    )(u_all, cs128, w_lat, w_ctx)


def _tail_kernel(x_ref, at_ref, cv_ref, fn_ref, gt_ref, gate_ref, shift_ref, scale_ref,
                 gpost_ref, gpre_ref, wa_ref, wc_ref, wf_ref, wo_ref, rw_ref, rb_ref,
                 xo_ref, h_ref, ti_ref, tp_ref):
    ya = _dot(at_ref[0], wa_ref[...])
    yc = _dot(cv_ref[0], wc_ref[...])
    yf = _dot(fn_ref[0], wf_ref[...])
    g0 = _sigmoid(gt_ref[0, :, 0:D_MODEL].astype(F32))
    g1 = _sigmoid(gt_ref[0, :, D_MODEL:2 * D_MODEL].astype(F32))
    g2 = _sigmoid(gt_ref[0, :, 2 * D_MODEL:3 * D_MODEL].astype(F32))
    merged = g0 * ya + g1 * yc + g2 * yf
    mix = _dot(merged.astype(BF16), wo_ref[...])
    xn = x_ref[0] + gate_ref[0] * _rms(mix, gpost_ref[...])
    xo_ref[0] = xn
    h = _rms(xn, gpre_ref[...]) * (1.0 + scale_ref[0]) + shift_ref[0]
    h_ref[0] = h

    logits = jnp.dot(h, rw_ref[...], precision=HIGHEST, preferred_element_type=F32) + rb_ref[...]
    lane = lax.broadcasted_iota(jnp.int32, logits.shape, 1)
    lane_f = lane.astype(F32)
    work = logits
    vals, idxs = [], []
    for _ in range(TOP_K):
        m = jnp.max(work, axis=-1, keepdims=True)
        idx = jnp.min(jnp.where(work == m, lane_f, float(LANES)), axis=-1, keepdims=True)
        vals.append(m)
        idxs.append(idx)
        work = jnp.where(lane_f == idx, 2.0 * NEG_BIG, work)
    es = [jnp.exp(v - vals[0]) for v in vals]
    denom = es[0] + es[1] + es[2] + es[3]
    ti = jnp.zeros(logits.shape, F32)
    tp = jnp.zeros(logits.shape, F32)
    for k in range(TOP_K):
        ti = jnp.where(lane == k, idxs[k], ti)
        tp = jnp.where(lane == k, es[k] / denom, tp)
    ti_ref[0] = ti.astype(jnp.int32)
    tp_ref[0] = tp


def _mixer_tail(x_all, attn_o, conv_h, fnet_f, u_all, mod, g_post, g_pre, wa, wc, wf, wo, rw, rb,
                *, batch, n_tiles, n_lat_tiles, rows):
    tile = lambda w: pl.BlockSpec((1, TOKEN_TILE, w), lambda b, t: (b, t, 0))
    vec = pl.BlockSpec((1, D_MODEL), lambda b, t: (0, 0))
    const = lambda a: _resident(a.shape, lambda b, t: (0, 0))
    out_tile = lambda w, dt: jax.ShapeDtypeStruct((batch, rows, w), dt)
    return pl.pallas_call(
        _tail_kernel,
        grid=(batch, n_tiles),
        in_specs=[
            tile(D_MODEL), tile(Q_DIM), tile(CONV_WIDTH), tile(FNET_WIDTH),
            pl.BlockSpec((1, TOKEN_TILE, 3 * D_MODEL), lambda b, t: (b, t, OFF_GATE // (3 * D_MODEL))),
            _mod_spec(2, n_lat_tiles, batch), _mod_spec(3, n_lat_tiles, batch),
            _mod_spec(4, n_lat_tiles, batch),
            vec, vec, const(wa), const(wc), const(wf), const(wo), const(rw), const(rb),
        ],
        out_specs=[tile(D_MODEL), tile(D_MODEL), tile(LANES), tile(LANES)],
        out_shape=[out_tile(D_MODEL, F32), out_tile(D_MODEL, F32),
                   out_tile(LANES, jnp.int32), out_tile(LANES, F32)],
        compiler_params=_params(("arbitrary", "arbitrary"), VMEM_LIMIT),
        name="mixer_tail",
    )(x_all, attn_o, conv_h, fnet_f, u_all, mod, mod, mod,
      g_post.reshape(1, D_MODEL), g_pre.reshape(1, D_MODEL), wa, wc, wf, wo, rw, rb)


def _route(top_i, *, batch, rows):
    e_flat = top_i[:, :, :TOP_K].reshape(-1)
    n_slots = batch * rows * TOP_K
    _, slot_sorted = lax.sort((e_flat, jnp.arange(n_slots, dtype=jnp.int32)), num_keys=1, is_stable=True)
    counts = jnp.sum((e_flat[:, None] == jnp.arange(N_EXPERTS, dtype=jnp.int32)[None, :]).astype(jnp.int32), axis=0)
    starts = jnp.cumsum(counts) - counts
    tiles_e = (counts + MOE_TILE - 1) // MOE_TILE
    tile_ends = jnp.cumsum(tiles_e)
    n_used = tile_ends[-1]
    n_tiles = n_slots // MOE_TILE + N_EXPERTS
    tile = jnp.arange(n_tiles, dtype=jnp.int32)
    tile_c = jnp.minimum(tile, n_used - 1)
    expert = jnp.minimum(jnp.searchsorted(tile_ends, tile_c, side="right").astype(jnp.int32), N_EXPERTS - 1)
    local = (tile_c - (tile_ends - tiles_e)[expert]) * MOE_TILE
    n_valid = jnp.where(tile < n_used, jnp.clip(counts[expert] - local, 0, MOE_TILE), 0)
    offset = starts[expert] + local
    padded = jnp.concatenate([slot_sorted, jnp.zeros((MOE_TILE,), jnp.int32)])
    window = jax.vmap(lambda o: lax.dynamic_slice(padded, (o,), (MOE_TILE,)))(offset)
    r = jnp.arange(MOE_TILE, dtype=jnp.int32)[None, :]
    valid = r < n_valid[:, None]
    dummy_row = n_slots + r
    dst = jnp.where(valid, window, dummy_row)
    src = jnp.where(valid, window // TOP_K, 0)
    return (expert, n_used.reshape(1).astype(jnp.int32),
            src.reshape(n_tiles, 1, MOE_TILE), dst.reshape(n_tiles, 1, MOE_TILE))


def _moe_kernel(expert_ref, n_used_ref, src_ref, dst_ref, h_ref, wg_ref, wu_ref, wd_ref,
                bg_ref, bu_ref, bd_ref, y_ref, xbuf, ybuf, wg_bf, wu_bf, wd_bf, gsem, ssem):
    i = pl.program_id(0)

    @pl.when(i == 0)
    def _():
        ybuf[...] = jnp.zeros(ybuf.shape, F32)
        init = pltpu.make_async_copy(ybuf, y_ref.at[pl.ds(y_ref.shape[0] - MOE_TILE, MOE_TILE), :], ssem)
        init.start()
        init.wait()

    @pl.when(i < n_used_ref[0])
    def _():
        for r in range(MOE_TILE):
            pltpu.make_async_copy(h_ref.at[pl.ds(src_ref[0, 0, r], 1), :],
                                  xbuf.at[pl.ds(r, 1), :], gsem).start()

        changed = jnp.logical_or(i == 0, expert_ref[i] != expert_ref[jnp.maximum(i - 1, 0)])

        @pl.when(changed)
        def _():
            rows = 128

            def cast(c, carry):
                sl = pl.ds(pl.multiple_of(c * rows, rows), rows)
                wg_bf[sl, :] = wg_ref[0, sl, :].astype(BF16)
                wu_bf[sl, :] = wu_ref[0, sl, :].astype(BF16)
                wd_bf[sl, :] = wd_ref[0, sl, :].astype(BF16)
                return carry
            lax.fori_loop(0, D_MODEL // rows, cast, 0)

        pltpu.make_async_copy(h_ref.at[pl.ds(0, MOE_TILE), :], xbuf, gsem).wait()
        x = xbuf[...].astype(BF16)
        g = jnp.minimum(_dot(x, wg_bf[...]) + bg_ref[0], SWIGLU_LIMIT)
        u = jnp.clip(_dot(x, wu_bf[...]) + bu_ref[0], -SWIGLU_LIMIT, SWIGLU_LIMIT)
        act = (u + 1.0) * (g * _sigmoid(SWIGLU_ALPHA * g))
        ybuf[...] = _dot(act.astype(BF16), wd_bf[...]) + bd_ref[0]
        for r in range(MOE_TILE):
            pltpu.make_async_copy(ybuf.at[pl.ds(r, 1), :],
                                  y_ref.at[pl.ds(dst_ref[0, 0, r], 1), :], ssem).start()
        pltpu.make_async_copy(ybuf, y_ref.at[pl.ds(0, MOE_TILE), :], ssem).wait()


def _moe(h_rows, routing, w_gate, b_gate, w_up, b_up, w_down, b_down):
    expert, n_used, src, dst = routing
    n_tiles = src.shape[0]
    n_rows = h_rows.shape[0]
    wspec = pl.BlockSpec((1, D_MODEL, D_MODEL), lambda i, ex, nu: (ex[i], 0, 0))
    bspec = pl.BlockSpec((1, 1, D_MODEL), lambda i, ex, nu: (ex[i], 0, 0))
    ids = pl.BlockSpec((1, 1, MOE_TILE), lambda i, ex, nu: (i, 0, 0), memory_space=pltpu.SMEM)
    bias = lambda b: b.reshape(N_EXPERTS, 1, D_MODEL)
    return pl.pallas_call(
        _moe_kernel,
        grid_spec=pltpu.PrefetchScalarGridSpec(
            num_scalar_prefetch=2,
            grid=(n_tiles,),
            in_specs=[ids, ids, pl.BlockSpec(memory_space=pl.ANY), wspec, wspec, wspec, bspec, bspec, bspec],
            out_specs=pl.BlockSpec(memory_space=pl.ANY),
            scratch_shapes=[
                pltpu.VMEM((MOE_TILE, D_MODEL), F32), pltpu.VMEM((MOE_TILE, D_MODEL), F32),
                pltpu.VMEM((D_MODEL, D_MODEL), BF16), pltpu.VMEM((D_MODEL, D_MODEL), BF16),
                pltpu.VMEM((D_MODEL, D_MODEL), BF16),
                pltpu.SemaphoreType.DMA(()), pltpu.SemaphoreType.DMA(()),
            ],
        ),
        out_shape=jax.ShapeDtypeStruct((n_rows * TOP_K + MOE_TILE, D_MODEL), F32),
        compiler_params=_params(("arbitrary",), VMEM_LIMIT),
        name="expert_ffn",
    )(expert, n_used, src, dst, h_rows, w_gate, w_up, w_down, bias(b_gate), bias(b_up), bias(b_down))


def _combine_kernel(x_ref, y_ref, p_ref, gate_ref, g_ref, o_ref):
    p = p_ref[0]
    y = p[:, 0:1] * y_ref[:, 0:D_MODEL]
    for k in range(1, TOP_K):
        y = y + p[:, k:k + 1] * y_ref[:, k * D_MODEL:(k + 1) * D_MODEL]
    o_ref[0] = x_ref[0] + gate_ref[0] * _rms(y, g_ref[...])


def _combine(x_new, y_slots, top_p, mod, g_post, *, batch, n_tiles, n_lat_tiles, rows):
    tiles_per_batch = rows // TOKEN_TILE
    y2 = y_slots.reshape(-1, TOP_K * D_MODEL)
    return pl.pallas_call(
        _combine_kernel,
        grid=(batch, n_tiles),
        in_specs=[
            pl.BlockSpec((1, TOKEN_TILE, D_MODEL), lambda b, t: (b, t, 0)),
            pl.BlockSpec((TOKEN_TILE, TOP_K * D_MODEL), lambda b, t: (b * tiles_per_batch + t, 0)),
            pl.BlockSpec((1, TOKEN_TILE, LANES), lambda b, t: (b, t, 0)),
            _mod_spec(5, n_lat_tiles, batch),
            pl.BlockSpec((1, D_MODEL), lambda b, t: (0, 0)),
        ],
        out_specs=pl.BlockSpec((1, TOKEN_TILE, D_MODEL), lambda b, t: (b, t, 0)),
        out_shape=jax.ShapeDtypeStruct((batch, rows, D_MODEL), F32),
        compiler_params=_params(("arbitrary", "arbitrary"), VMEM_LIMIT),
        name="combine",
    )(x_new, y2, top_p, mod, g_post.reshape(1, D_MODEL))


def _rope_tables(n_lat, n_ctx):
    pos = jnp.arange(n_lat, dtype=jnp.int32)
    row = (pos // GRID_W).astype(F32)
    col = (pos % GRID_W).astype(F32)
    n_freq = HEAD_DIM // 4
    inv_freq = ROPE_THETA ** (-jnp.arange(n_freq, dtype=F32) / n_freq)
    ang_r = row[:, None] * inv_freq[None, :]
    ang_c = col[:, None] * inv_freq[None, :]
    ang = jnp.concatenate([ang_r, ang_r, ang_c, ang_c], axis=-1)
    sign = jnp.where((jnp.arange(HEAD_DIM) % 64) < 32, -1.0, 1.0).astype(F32)
    cos = jnp.concatenate([jnp.cos(ang), jnp.ones((n_ctx, HEAD_DIM), F32)], axis=0)
    sin = jnp.concatenate([jnp.sin(ang) * sign[None, :], jnp.zeros((n_ctx, HEAD_DIM), F32)], axis=0)
    return cos, sin


def kernel(x, c, ctx, c_ctx, w_ada, b_ada, g_pre_mix, g_post_mix, g_pre_ffn, g_post_ffn, w_in, q_norm_g, k_norm_g, w_attn_o, conv_dw_w, conv_dw_b, conv_ln_g, conv_ln_b, w_conv_o, w_fnet_o, w_out, router_w, router_b, w_gate, b_gate, w_up, b_up, w_down, b_down):
    batch, n_lat, _ = x.shape
    n_ctx = ctx.shape[1]
    depth = w_ada.shape[0]
    assert n_ctx == TOKEN_TILE and n_lat % TOKEN_TILE == 0 and n_lat % GRID_W == 0
    assert batch + 1 <= MOD_ROWS and x.shape[2] == D_MODEL
    rows = n_lat + n_ctx
    n_lat_tiles = n_lat // TOKEN_TILE
    ctx_tile = n_lat_tiles

    c_rows = jnp.concatenate(
        [c, c_ctx[None, :], jnp.zeros((MOD_ROWS - batch - 1, D_MODEL), F32)], axis=0)
    mod_all = _adaln(c_rows, w_ada, b_ada).reshape(depth, MOD_ROWS * N_MOD, 1, D_MODEL)
    cos, sin = _rope_tables(n_lat, n_ctx)
    cs128 = _channel_dft()
    w_dft_lat = _dft_tables(n_lat)
    w_dft_ctx = _dft_tables(n_ctx)
    pad_lanes = LANES - N_EXPERTS

    x_all = jnp.concatenate([x, ctx], axis=1)
    for l in range(depth):
        last = l == depth - 1
        with_ctx = not last
        n_tiles = n_lat_tiles + (1 if with_ctx else 0)
        rows_l = n_tiles * TOKEN_TILE
        mod = mod_all[l]
        w_in_bf = w_in[l].astype(BF16)

        u_all = _inproj(x_all, mod, g_pre_mix[l], w_in_bf, batch=batch, first_tile=0,
                        n_tiles=n_tiles, n_lat_tiles=n_lat_tiles, out_rows=rows_l)
        if with_ctx:
            kv_ctx, kc_blk = u_all, (ctx_tile, OFF_K // HEAD_DIM, OFF_V // HEAD_DIM)
        else:
            kv_ctx = _inproj(x_all, mod, g_pre_mix[l], w_in_bf[:, OFF_K:OFF_CONV], batch=batch,
                             first_tile=ctx_tile, n_tiles=1, n_lat_tiles=n_lat_tiles, out_rows=n_ctx)
            kc_blk = (0, 0, KV_DIM // HEAD_DIM)
        attn_o = _attention(u_all, kv_ctx, kc_blk, cos, sin, q_norm_g[l], k_norm_g[l], batch=batch,
                            n_lat=n_lat, n_ctx=n_ctx, n_q_tiles=n_tiles, out_rows=rows_l)
        conv_h = _conv_branch(u_all, conv_dw_w[l], conv_dw_b[l], conv_ln_g[l], conv_ln_b[l],
                              batch=batch, n_lat=n_lat, n_ctx=n_ctx, with_ctx=with_ctx)
        fnet_f = _fnet_branch(u_all, cs128, w_dft_lat, w_dft_ctx, batch=batch, n_lat=n_lat,
                              n_ctx=n_ctx, with_ctx=with_ctx)
        rw = jnp.concatenate([router_w[l], jnp.zeros((D_MODEL, pad_lanes), F32)], axis=1)
        rb = jnp.concatenate([router_b[l], jnp.full((pad_lanes,), NEG_BIG, F32)]).reshape(1, LANES)
        x_new, h_ffn, top_i, top_p = _mixer_tail(
            x_all, attn_o, conv_h, fnet_f, u_all, mod, g_post_mix[l], g_pre_ffn[l],
            w_attn_o[l].astype(BF16), w_conv_o[l].astype(BF16), w_fnet_o[l].astype(BF16),
            w_out[l].astype(BF16), rw, rb, batch=batch, n_tiles=n_tiles, n_lat_tiles=n_lat_tiles,
            rows=rows_l)
        routing = _route(top_i, batch=batch, rows=rows_l)
        y_slots = _moe(h_ffn.reshape(batch * rows_l, D_MODEL), routing,
                       w_gate[l], b_gate[l], w_up[l], b_up[l], w_down[l], b_down[l])
        x_all = _combine(x_new, y_slots, top_p, mod, g_post_ffn[l], batch=batch, n_tiles=n_tiles,
                         n_lat_tiles=n_lat_tiles, rows=rows_l)
    return x_all
```

```python
import functools

import jax
import jax.numpy as jnp
import numpy as np
from jax import lax
from jax.experimental import pallas as pl
from jax.experimental.pallas import tpu as pltpu

F32 = jnp.float32
BF16 = jnp.bfloat16
HIGHEST = lax.Precision.HIGHEST

D_MODEL = 1024
GRID_W = 64
EPS = 1e-6
N_MOD = 6
HEAD_DIM = 128
N_HEADS = 8
N_KV_HEADS = 2
GQA_GROUP = N_HEADS // N_KV_HEADS
ROPE_THETA = 10000.0
CONV_WIDTH = 512
CONV_KERNEL = 31
CONV_PAD = 16
FNET_GROUPS = 4
FNET_GROUP_DIM = 128
FNET_WIDTH = FNET_GROUPS * FNET_GROUP_DIM
N_EXPERTS = 32
TOP_K = 4
SWIGLU_LIMIT = 7.0
SWIGLU_ALPHA = 1.702

Q_DIM = N_HEADS * HEAD_DIM
KV_DIM = N_KV_HEADS * HEAD_DIM
OFF_K = Q_DIM
OFF_V = OFF_K + KV_DIM
OFF_CONV = OFF_V + KV_DIM
OFF_FNET = OFF_CONV + 2 * CONV_WIDTH
OFF_GATE = OFF_FNET + FNET_WIDTH
IN_COLS = OFF_GATE + 3 * D_MODEL

LANES = 128
TOKEN_TILE = 256
MOE_TILE = 512
MOD_ROWS = 24
CONV_ROWS = 32
NEG_BIG = -1e30
VMEM_LIMIT = 56 * 1024 * 1024


def _params(sem, vmem=None):
    return pltpu.CompilerParams(dimension_semantics=sem, vmem_limit_bytes=vmem)


def _resident(shape, index_map):
    return pl.BlockSpec(shape, index_map, pipeline_mode=pl.Buffered(1))


def _rms(x, g):
    return x * lax.rsqrt(jnp.mean(x * x, axis=-1, keepdims=True) + EPS) * g


def _sigmoid(x):
    return 1.0 / (1.0 + jnp.exp(-x))


def _dot(a, b):
    return jnp.dot(a, b, preferred_element_type=F32)


def _adaln_kernel(c_ref, w_ref, b_ref, o_ref):
    c = c_ref[...]
    s = c * _sigmoid(c)
    o_ref[0] = jnp.dot(s, w_ref[0], precision=HIGHEST, preferred_element_type=F32) + b_ref[0]


def _adaln(c_rows, w_ada, b_ada):
    depth, _, n = w_ada.shape
    tn = 1024
    return pl.pallas_call(
        _adaln_kernel,
        grid=(depth, n // tn),
        in_specs=[
            pl.BlockSpec((MOD_ROWS, D_MODEL), lambda l, j: (0, 0)),
            pl.BlockSpec((1, D_MODEL, tn), lambda l, j: (l, 0, j)),
            pl.BlockSpec((1, 1, tn), lambda l, j: (l, 0, j)),
        ],
        out_specs=pl.BlockSpec((1, MOD_ROWS, tn), lambda l, j: (l, 0, j)),
        out_shape=jax.ShapeDtypeStruct((depth, MOD_ROWS, n), F32),
        compiler_params=_params(("arbitrary", "arbitrary")),
        name="adaln",
    )(c_rows, w_ada, b_ada.reshape(depth, 1, n))


def _mod_spec(j, n_lat_tiles, ctx_row):
    def index_map(b, t):
        row = jnp.where(t >= n_lat_tiles, ctx_row, b)
        return (row * N_MOD + j, 0, 0)
    return pl.BlockSpec((1, 1, D_MODEL), index_map)


def _inproj_kernel(x_ref, shift_ref, scale_ref, g_ref, w_ref, o_ref):
    h = _rms(x_ref[0], g_ref[...]) * (1.0 + scale_ref[0]) + shift_ref[0]
    hb = h.astype(BF16)
    n = w_ref.shape[1]
    nc = min(n, 1024)
    for j in range(n // nc):
        o_ref[0, :, j * nc:(j + 1) * nc] = _dot(hb, w_ref[:, j * nc:(j + 1) * nc]).astype(BF16)


def _inproj(x_all, mod, g_pre, w_bf, *, batch, first_tile, n_tiles, n_lat_tiles, out_rows):
    n = w_bf.shape[1]
    return pl.pallas_call(
        _inproj_kernel,
        grid=(batch, n_tiles),
        in_specs=[
            pl.BlockSpec((1, TOKEN_TILE, D_MODEL), lambda b, t: (b, t + first_tile, 0)),
            _mod_spec(0, n_lat_tiles - first_tile, batch),
            _mod_spec(1, n_lat_tiles - first_tile, batch),
            pl.BlockSpec((1, D_MODEL), lambda b, t: (0, 0)),
            _resident((D_MODEL, n), lambda b, t: (0, 0)),
        ],
        out_specs=pl.BlockSpec((1, TOKEN_TILE, n), lambda b, t: (b, t, 0)),
        out_shape=jax.ShapeDtypeStruct((batch, out_rows, n), BF16),
        compiler_params=_params(("arbitrary", "arbitrary"), VMEM_LIMIT),
        name="inproj",
    )(x_all, mod, mod, g_pre.reshape(1, D_MODEL), w_bf)


def _rope(t, cos, sin_signed, first_half):
    rot = jnp.where(first_half, pltpu.roll(t, HEAD_DIM - 32, 1), pltpu.roll(t, 32, 1))
    return t * cos + rot * sin_signed


def _attn_kernel(q_ref, kl_ref, vl_ref, kc_ref, vc_ref, cosq_ref, sinq_ref, cosk_ref, sink_ref,
                 qg_ref, kg_ref, o_ref, kp_ref, vp_ref, *, n_lat, n_ctx, n_lat_tiles, ctx_queries):
    qi = pl.program_id(2)
    lane = lax.broadcasted_iota(jnp.int32, (1, HEAD_DIM), 1)
    first_half = (lane & 32) == 0

    @pl.when(qi == 0)
    def _():
        kl = _rms(kl_ref[0].astype(F32), kg_ref[...])
        kp_ref[0:n_lat, :] = _rope(kl, cosk_ref[0:n_lat, :], sink_ref[0:n_lat, :], first_half).astype(BF16)
        kp_ref[n_lat:n_lat + n_ctx, :] = _rms(kc_ref[0].astype(F32), kg_ref[...]).astype(BF16)
        vp_ref[0:n_lat, :] = vl_ref[0]
        vp_ref[n_lat:n_lat + n_ctx, :] = vc_ref[0]

    scale = HEAD_DIM ** -0.5
    if ctx_queries:
        key_is_latent = lax.broadcasted_iota(jnp.int32, (1, n_lat + n_ctx), 1) < n_lat
        bias = jnp.where(key_is_latent, jnp.where(qi >= n_lat_tiles, NEG_BIG, 0.0), 0.0)
    for g in range(GQA_GROUP):
        q = _rms(q_ref[0, :, g * HEAD_DIM:(g + 1) * HEAD_DIM].astype(F32), qg_ref[...])
        q = _rope(q, cosq_ref[...], sinq_ref[...], first_half) * scale
        s = lax.dot_general(q.astype(BF16), kp_ref[...], (((1,), (1,)), ((), ())),
                            preferred_element_type=F32)
        if ctx_queries:
            s = s + bias
        m = jnp.max(s, axis=-1, keepdims=True)
        p = jnp.exp(s - m)
        l = jnp.sum(p, axis=-1, keepdims=True)
        o = _dot(p.astype(BF16), vp_ref[...])
        o_ref[0, :, g * HEAD_DIM:(g + 1) * HEAD_DIM] = (o / l).astype(BF16)


def _attention(u_all, kv_ctx, kc_blk, cos, sin_signed, q_gain, k_gain, *, batch, n_lat, n_ctx,
               n_q_tiles, out_rows):
    n_lat_tiles = n_lat // TOKEN_TILE
    kcol = OFF_K // HEAD_DIM
    vcol = OFF_V // HEAD_DIM
    kc_row, kc_kcol, kc_vcol = kc_blk
    kernel = functools.partial(_attn_kernel, n_lat=n_lat, n_ctx=n_ctx, n_lat_tiles=n_lat_tiles,
                               ctx_queries=n_q_tiles > n_lat_tiles)
    width = GQA_GROUP * HEAD_DIM
    return pl.pallas_call(
        kernel,
        grid=(batch, N_KV_HEADS, n_q_tiles),
        in_specs=[
            pl.BlockSpec((1, TOKEN_TILE, width), lambda b, h, t: (b, t, h)),
            pl.BlockSpec((1, n_lat, HEAD_DIM), lambda b, h, t: (b, 0, kcol + h)),
            pl.BlockSpec((1, n_lat, HEAD_DIM), lambda b, h, t: (b, 0, vcol + h)),
            pl.BlockSpec((1, n_ctx, HEAD_DIM), lambda b, h, t: (b, kc_row, kc_kcol + h)),
            pl.BlockSpec((1, n_ctx, HEAD_DIM), lambda b, h, t: (b, kc_row, kc_vcol + h)),
            pl.BlockSpec((TOKEN_TILE, HEAD_DIM), lambda b, h, t: (t, 0)),
            pl.BlockSpec((TOKEN_TILE, HEAD_DIM), lambda b, h, t: (t, 0)),
            pl.BlockSpec((n_lat + n_ctx, HEAD_DIM), lambda b, h, t: (0, 0)),
            pl.BlockSpec((n_lat + n_ctx, HEAD_DIM), lambda b, h, t: (0, 0)),
            pl.BlockSpec((1, HEAD_DIM), lambda b, h, t: (0, 0)),
            pl.BlockSpec((1, HEAD_DIM), lambda b, h, t: (0, 0)),
        ],
        out_specs=pl.BlockSpec((1, TOKEN_TILE, width), lambda b, h, t: (b, t, h)),
        out_shape=jax.ShapeDtypeStruct((batch, out_rows, Q_DIM), BF16),
        scratch_shapes=[pltpu.VMEM((n_lat + n_ctx, HEAD_DIM), BF16),
                        pltpu.VMEM((n_lat + n_ctx, HEAD_DIM), BF16)],
        compiler_params=_params(("arbitrary", "arbitrary", "arbitrary"), VMEM_LIMIT),
        name="attention",
    )(u_all, u_all, u_all, kv_ctx, kv_ctx, cos, sin_signed, cos, sin_signed,
      q_gain.reshape(1, HEAD_DIM), k_gain.reshape(1, HEAD_DIM))


def _conv_segment(a_ref, g_ref, w_ref, b_ref, lng_ref, lnb_ref, o_ref, hp_ref, row0, n_rows):
    zeros = jnp.zeros((CONV_PAD, CONV_WIDTH), F32)
    hp_ref[0:CONV_PAD, :] = zeros
    hp_ref[CONV_PAD + n_rows:2 * CONV_PAD + n_rows, :] = zeros
    chunk = 256

    def glu(c, carry):
        src = pl.ds(pl.multiple_of(row0 + c * chunk, chunk), chunk)
        a = a_ref[0, src, :].astype(F32)
        g = g_ref[0, src, :].astype(F32)
        hp_ref[pl.ds(pl.multiple_of(CONV_PAD + c * chunk, CONV_PAD), chunk), :] = a * _sigmoid(g)
        return carry
    lax.fori_loop(0, n_rows // chunk, glu, 0)

    slab_rows = CONV_ROWS + 2 * CONV_PAD
    half = CONV_KERNEL // 2
    shifts = sorted((k - half + CONV_PAD for k in range(CONV_KERNEL)), key=lambda s: (s % 8, s))

    def conv(c, carry):
        r = pl.multiple_of(c * CONV_ROWS, CONV_ROWS)
        slab = hp_ref[pl.ds(r, slab_rows), :]
        acc = jnp.zeros((CONV_ROWS, CONV_WIDTH), F32) + b_ref[...]
        rotated, residue = slab, 0
        for shift in shifts:
            if shift % 8 != residue:
                residue = shift % 8
                rotated = pltpu.roll(slab, slab_rows - residue, 0)
            base = shift - residue
            k = shift + half - CONV_PAD
            acc = acc + rotated[base:base + CONV_ROWS, :] * w_ref[k:k + 1, :]
        mu = jnp.mean(acc, axis=-1, keepdims=True)
        xc = acc - mu
        var = jnp.mean(xc * xc, axis=-1, keepdims=True)
        y = xc * lax.rsqrt(var + EPS) * lng_ref[...] + lnb_ref[...]
        dst = pl.ds(pl.multiple_of(row0 + c * CONV_ROWS, CONV_ROWS), CONV_ROWS)
        o_ref[0, dst, :] = (y * _sigmoid(y)).astype(BF16)
        return carry
    lax.fori_loop(0, n_rows // CONV_ROWS, conv, 0)


def _conv_kernel(a_ref, g_ref, w_ref, b_ref, lng_ref, lnb_ref, o_ref, hp_ref, *, n_lat, n_ctx, with_ctx):
    _conv_segment(a_ref, g_ref, w_ref, b_ref, lng_ref, lnb_ref, o_ref, hp_ref, 0, n_lat)
    if with_ctx:
        _conv_segment(a_ref, g_ref, w_ref, b_ref, lng_ref, lnb_ref, o_ref, hp_ref, n_lat, n_ctx)


def _conv_branch(u_all, dw_w, dw_b, ln_g, ln_b, *, batch, n_lat, n_ctx, with_ctx):
    rows = u_all.shape[1]
    kernel = functools.partial(_conv_kernel, n_lat=n_lat, n_ctx=n_ctx, with_ctx=with_ctx)
    acol = OFF_CONV // CONV_WIDTH
    vec = lambda v: v.reshape(1, CONV_WIDTH)
    small = pl.BlockSpec((1, CONV_WIDTH), lambda b: (0, 0))
    return pl.pallas_call(
        kernel,
        grid=(batch,),
        in_specs=[
            pl.BlockSpec((1, rows, CONV_WIDTH), lambda b: (b, 0, acol)),
            pl.BlockSpec((1, rows, CONV_WIDTH), lambda b: (b, 0, acol + 1)),
            pl.BlockSpec((CONV_KERNEL, CONV_WIDTH), lambda b: (0, 0)),
            small, small, small,
        ],
        out_specs=pl.BlockSpec((1, rows, CONV_WIDTH), lambda b: (b, 0, 0)),
        out_shape=jax.ShapeDtypeStruct((batch, rows, CONV_WIDTH), BF16),
        scratch_shapes=[pltpu.VMEM((n_lat + 2 * CONV_PAD, CONV_WIDTH), F32)],
        compiler_params=_params(("arbitrary",), VMEM_LIMIT),
        name="conv_branch",
    )(u_all, u_all, dw_w, vec(dw_b), vec(ln_g), vec(ln_b))


def _dft_tables(n):
    t = lax.broadcasted_iota(jnp.int32, (n, n), 0)
    k = lax.broadcasted_iota(jnp.int32, (n, n), 1)
    ang = ((t * k) % n).astype(F32) * (2.0 * np.pi / n)
    scale = (n * FNET_GROUP_DIM) ** -0.5
    return jnp.concatenate([jnp.cos(ang) * scale, -jnp.sin(ang) * scale], axis=1).astype(BF16)


def _channel_dft():
    t = lax.broadcasted_iota(jnp.int32, (FNET_GROUP_DIM, FNET_GROUP_DIM), 0)
    k = lax.broadcasted_iota(jnp.int32, (FNET_GROUP_DIM, FNET_GROUP_DIM), 1)
    ang = ((t * k) % FNET_GROUP_DIM).astype(F32) * (2.0 * np.pi / FNET_GROUP_DIM)
    return jnp.concatenate([jnp.cos(ang), jnp.sin(ang)], axis=1).astype(BF16)


def _fnet_kernel(x_ref, cs_ref, wl_ref, wc_ref, o_ref, abl_ref, abc_ref, *, n_lat, n_ctx, with_ctx):
    t = pl.program_id(1)
    n_lat_tiles = n_lat // TOKEN_TILE

    def channel_pass(row0, n, ab_ref):
        for grp in range(FNET_GROUPS):
            cols = slice(grp * FNET_GROUP_DIM, (grp + 1) * FNET_GROUP_DIM)
            y = _dot(x_ref[0, row0:row0 + n, cols], cs_ref[...])
            ab_ref[0:n, cols] = y[:, :FNET_GROUP_DIM].astype(BF16)
            ab_ref[n:2 * n, cols] = y[:, FNET_GROUP_DIM:].astype(BF16)

    @pl.when(t == 0)
    def _():
        channel_pass(0, n_lat, abl_ref)
        if with_ctx:
            channel_pass(n_lat, n_ctx, abc_ref)

    @pl.when(t < n_lat_tiles)
    def _():
        o_ref[0] = _dot(wl_ref[...], abl_ref[...]).astype(BF16)

    if with_ctx:
        @pl.when(t >= n_lat_tiles)
        def _():
            o_ref[0] = _dot(wc_ref[...], abc_ref[...]).astype(BF16)


def _fnet_branch(u_all, cs128, w_lat, w_ctx, *, batch, n_lat, n_ctx, with_ctx):
    rows = u_all.shape[1]
    n_lat_tiles = n_lat // TOKEN_TILE
    n_tiles = n_lat_tiles + (1 if with_ctx else 0)
    kernel = functools.partial(_fnet_kernel, n_lat=n_lat, n_ctx=n_ctx, with_ctx=with_ctx)
    xcol = OFF_FNET // FNET_WIDTH
    return pl.pallas_call(
        kernel,
        grid=(batch, n_tiles),
        in_specs=[
            pl.BlockSpec((1, rows, FNET_WIDTH), lambda b, t: (b, 0, xcol)),
            pl.BlockSpec((FNET_GROUP_DIM, 2 * FNET_GROUP_DIM), lambda b, t: (0, 0)),
            pl.BlockSpec((TOKEN_TILE, 2 * n_lat), lambda b, t: (jnp.minimum(t, n_lat_tiles - 1), 0)),
            pl.BlockSpec((n_ctx, 2 * n_ctx), lambda b, t: (0, 0)),
        ],
        out_specs=pl.BlockSpec((1, TOKEN_TILE, FNET_WIDTH), lambda b, t: (b, t, 0)),
        out_shape=jax.ShapeDtypeStruct((batch, rows, FNET_WIDTH), BF16),
        scratch_shapes=[pltpu.VMEM((2 * n_lat, FNET_WIDTH), BF16),
                        pltpu.VMEM((2 * n_ctx, FNET_WIDTH), BF16)],
        compiler_params=_params(("arbitrary", "arbitrary"), VMEM_LIMIT),
        name="fnet_branch",
    )(u_all, cs128, w_lat, w_ctx)


def _tail_kernel(x_ref, at_ref, cv_ref, fn_ref, gt_ref, gate_ref, shift_ref, scale_ref,
                 gpost_ref, gpre_ref, wa_ref, wc_ref, wf_ref, wo_ref, rw_ref, rb_ref,
                 xo_ref, h_ref, ti_ref, tp_ref):
    ya = _dot(at_ref[0], wa_ref[...])
    yc = _dot(cv_ref[0], wc_ref[...])
    yf = _dot(fn_ref[0], wf_ref[...])
    g0 = _sigmoid(gt_ref[0, :, 0:D_MODEL].astype(F32))
    g1 = _sigmoid(gt_ref[0, :, D_MODEL:2 * D_MODEL].astype(F32))
    g2 = _sigmoid(gt_ref[0, :, 2 * D_MODEL:3 * D_MODEL].astype(F32))
    merged = g0 * ya + g1 * yc + g2 * yf
    mix = _dot(merged.astype(BF16), wo_ref[...])
    xn = x_ref[0] + gate_ref[0] * _rms(mix, gpost_ref[...])
    xo_ref[0] = xn
    h = _rms(xn, gpre_ref[...]) * (1.0 + scale_ref[0]) + shift_ref[0]
    h_ref[0] = h

    logits = jnp.dot(h, rw_ref[...], precision=HIGHEST, preferred_element_type=F32) + rb_ref[...]
    lane = lax.broadcasted_iota(jnp.int32, logits.shape, 1)
    lane_f = lane.astype(F32)
    work = logits
    vals, idxs = [], []
    for _ in range(TOP_K):
        m = jnp.max(work, axis=-1, keepdims=True)
        idx = jnp.min(jnp.where(work == m, lane_f, float(LANES)), axis=-1, keepdims=True)
        vals.append(m)
        idxs.append(idx)
        work = jnp.where(lane_f == idx, 2.0 * NEG_BIG, work)
    es = [jnp.exp(v - vals[0]) for v in vals]
    denom = es[0] + es[1] + es[2] + es[3]
    ti = jnp.zeros(logits.shape, F32)
    tp = jnp.zeros(logits.shape, F32)
    for k in range(TOP_K):
        ti = jnp.where(lane == k, idxs[k], ti)
        tp = jnp.where(lane == k, es[k] / denom, tp)
    ti_ref[0, 0] = jnp.transpose(ti)[0:8, :].astype(jnp.int32)
    tp_ref[0] = tp


def _mixer_tail(x_all, attn_o, conv_h, fnet_f, u_all, mod, g_post, g_pre, wa, wc, wf, wo, rw, rb,
                *, batch, n_tiles, n_lat_tiles, rows):
    tile = lambda w: pl.BlockSpec((1, TOKEN_TILE, w), lambda b, t: (b, t, 0))
    vec = pl.BlockSpec((1, D_MODEL), lambda b, t: (0, 0))
    const = lambda a: _resident(a.shape, lambda b, t: (0, 0))
    out_tile = lambda w, dt: jax.ShapeDtypeStruct((batch, rows, w), dt)
    return pl.pallas_call(
        _tail_kernel,
        grid=(batch, n_tiles),
        in_specs=[
            tile(D_MODEL), tile(Q_DIM), tile(CONV_WIDTH), tile(FNET_WIDTH),
            pl.BlockSpec((1, TOKEN_TILE, 3 * D_MODEL), lambda b, t: (b, t, OFF_GATE // (3 * D_MODEL))),
            _mod_spec(2, n_lat_tiles, batch), _mod_spec(3, n_lat_tiles, batch),
            _mod_spec(4, n_lat_tiles, batch),
            vec, vec, const(wa), const(wc), const(wf), const(wo), const(rw), const(rb),
        ],
        out_specs=[tile(D_MODEL), tile(D_MODEL),
                   pl.BlockSpec((1, 1, 8, TOKEN_TILE), lambda b, t: (b, t, 0, 0)), tile(LANES)],
        out_shape=[out_tile(D_MODEL, F32), out_tile(D_MODEL, F32),
                   jax.ShapeDtypeStruct((batch, n_tiles, 8, TOKEN_TILE), jnp.int32),
                   out_tile(LANES, F32)],
        compiler_params=_params(("arbitrary", "arbitrary"), VMEM_LIMIT),
        name="mixer_tail",
    )(x_all, attn_o, conv_h, fnet_f, u_all, mod, mod, mod,
      g_post.reshape(1, D_MODEL), g_pre.reshape(1, D_MODEL), wa, wc, wf, wo, rw, rb)


def _route(top_i, *, batch, rows):
    e_flat = top_i[:, :, :TOP_K, :].reshape(-1)
    n_tok = batch * rows
    n_slots = n_tok * TOP_K
    experts = jnp.arange(N_EXPERTS, dtype=jnp.int32)
    counts = jnp.sum((e_flat[:, None] == experts[None, :]).astype(jnp.int32), axis=0)
    pad = (-counts) % MOE_TILE
    filler_e = jnp.repeat(experts, MOE_TILE)
    filler_j = jnp.tile(jnp.arange(MOE_TILE, dtype=jnp.int32), N_EXPERTS)
    filler_key = jnp.where(filler_j < pad[filler_e], 2 * filler_e + 1, 2 * N_EXPERTS)
    keys = jnp.concatenate([2 * e_flat, filler_key])
    vals = jnp.concatenate([jnp.arange(n_slots, dtype=jnp.int32), n_slots + filler_j])
    _, table = lax.sort((keys, vals), num_keys=1, is_stable=True)
    n_tiles = n_slots // MOE_TILE + N_EXPERTS
    real = table < n_slots
    tok = (table // (TOP_K * TOKEN_TILE)) * TOKEN_TILE + table % TOKEN_TILE
    k = (table // TOKEN_TILE) % TOP_K
    src = jnp.where(real, tok, 0).reshape(n_tiles, 1, MOE_TILE)
    dst = jnp.where(real, k * n_tok + tok, table).reshape(n_tiles, 1, MOE_TILE)
    tile_ends = jnp.cumsum((counts + pad) // MOE_TILE)
    n_used = tile_ends[-1]
    tile = jnp.minimum(jnp.arange(n_tiles, dtype=jnp.int32), n_used - 1)
    expert = jnp.sum((tile[:, None] >= tile_ends[None, :]).astype(jnp.int32), axis=1)
    dummy = (n_slots + jnp.arange(MOE_TILE, dtype=jnp.int32)).reshape(1, 1, MOE_TILE)
    prev_dst = jnp.concatenate([dummy, dst[:-1]], axis=0)
    return expert, n_used.reshape(1).astype(jnp.int32), src, prev_dst


def _moe_kernel(expert_ref, n_used_ref, src_ref, next_src_ref, prev_dst_ref, h_ref, wg_ref, wu_ref,
                wd_ref, bg_ref, bu_ref, bd_ref, y_ref, x0, x1, y0, y1, wg_bf, wu_bf, wd_bf, gsem, ssem):
    i = pl.program_id(0)
    n_used = n_used_ref[0]

    def start_gather(ids_ref, xbuf, sem):
        for r in range(MOE_TILE):
            pltpu.make_async_copy(h_ref.at[pl.ds(ids_ref[0, 0, r], 1), :],
                                  xbuf.at[pl.ds(r, 1), :], sem).start()

    def wait_gather(xbuf, sem):
        pltpu.make_async_copy(h_ref.at[pl.ds(0, MOE_TILE), :], xbuf, sem).wait()

    def start_scatter(ybuf, sem):
        for r in range(MOE_TILE):
            pltpu.make_async_copy(ybuf.at[pl.ds(r, 1), :],
                                  y_ref.at[pl.ds(prev_dst_ref[0, 0, r], 1), :], sem).start()

    def wait_scatter(ybuf, sem):
        pltpu.make_async_copy(ybuf, y_ref.at[pl.ds(0, MOE_TILE), :], sem).wait()

    @pl.when(i == 0)
    def _():
        y0[...] = jnp.zeros(y0.shape, F32)
        y1[...] = jnp.zeros(y1.shape, F32)
        start_gather(src_ref, x0, gsem.at[0])

    changed = jnp.logical_or(i == 0, expert_ref[i] != expert_ref[jnp.maximum(i - 1, 0)])

    @pl.when(jnp.logical_and(i < n_used, changed))
    def _():
        rows = 128

        def cast(c, carry):
            sl = pl.ds(pl.multiple_of(c * rows, rows), rows)
            wg_bf[sl, :] = wg_ref[0, sl, :].astype(BF16)
            wu_bf[sl, :] = wu_ref[0, sl, :].astype(BF16)
            wd_bf[sl, :] = wd_ref[0, sl, :].astype(BF16)
            return carry
        lax.fori_loop(0, D_MODEL // rows, cast, 0)

    def step(x_cur, y_cur, g_cur, s_cur, x_oth, y_oth, g_oth, s_oth):
        @pl.when(i <= n_used)
        def _():
            wait_gather(x_cur, g_cur)

            @pl.when(i > 0)
            def _():
                wait_scatter(y_cur, s_cur)

        @pl.when(i < n_used)
        def _():
            start_gather(next_src_ref, x_oth, g_oth)
            start_scatter(y_oth, s_oth)
            x = x_cur[...].astype(BF16)
            g = jnp.minimum(_dot(x, wg_bf[...]) + bg_ref[0], SWIGLU_LIMIT)
            u = jnp.clip(_dot(x, wu_bf[...]) + bu_ref[0], -SWIGLU_LIMIT, SWIGLU_LIMIT)
            act = (u + 1.0) * (g * _sigmoid(SWIGLU_ALPHA * g))
            y_cur[...] = _dot(act.astype(BF16), wd_bf[...]) + bd_ref[0]

        @pl.when(i == n_used)
        def _():
            start_scatter(y_oth, s_oth)
            wait_scatter(y_oth, s_oth)

    @pl.when(i % 2 == 0)
    def _():
        step(x0, y0, gsem.at[0], ssem.at[0], x1, y1, gsem.at[1], ssem.at[1])

    @pl.when(i % 2 == 1)
    def _():
        step(x1, y1, gsem.at[1], ssem.at[1], x0, y0, gsem.at[0], ssem.at[0])


def _moe(h_rows, routing, w_gate, b_gate, w_up, b_up, w_down, b_down):
    expert, n_used, src, prev_dst = routing
    n_tiles = src.shape[0]
    n_rows = h_rows.shape[0]
    wspec = pl.BlockSpec((1, D_MODEL, D_MODEL), lambda i, ex, nu: (ex[i], 0, 0))
    bspec = pl.BlockSpec((1, 1, D_MODEL), lambda i, ex, nu: (ex[i], 0, 0))
    ids = pl.BlockSpec((1, 1, MOE_TILE), lambda i, ex, nu: (i, 0, 0), memory_space=pltpu.SMEM)
    next_ids = pl.BlockSpec((1, 1, MOE_TILE), lambda i, ex, nu: (jnp.minimum(i + 1, n_tiles - 1), 0, 0),
                            memory_space=pltpu.SMEM)
    bias = lambda b: b.reshape(N_EXPERTS, 1, D_MODEL)
    tile_buf = pltpu.VMEM((MOE_TILE, D_MODEL), F32)
    weight_buf = pltpu.VMEM((D_MODEL, D_MODEL), BF16)
    return pl.pallas_call(
        _moe_kernel,
        grid_spec=pltpu.PrefetchScalarGridSpec(
            num_scalar_prefetch=2,
            grid=(n_tiles,),
            in_specs=[ids, next_ids, ids, pl.BlockSpec(memory_space=pl.ANY),
                      wspec, wspec, wspec, bspec, bspec, bspec],
            out_specs=pl.BlockSpec(memory_space=pl.ANY),
            scratch_shapes=[tile_buf, tile_buf, tile_buf, tile_buf, weight_buf, weight_buf, weight_buf,
                            pltpu.SemaphoreType.DMA((2,)), pltpu.SemaphoreType.DMA((2,))],
        ),
        out_shape=jax.ShapeDtypeStruct((n_rows * TOP_K + MOE_TILE, D_MODEL), F32),
        compiler_params=_params(("arbitrary",), VMEM_LIMIT),
        name="expert_ffn",
    )(expert, n_used, src, src, prev_dst, h_rows, w_gate, w_up, w_down,
      bias(b_gate), bias(b_up), bias(b_down))


def _combine_kernel(x_ref, y0_ref, y1_ref, y2_ref, y3_ref, p_ref, gate_ref, g_ref, o_ref):
    p = p_ref[0]
    y = p[:, 0:1] * y0_ref[...]
    for k, y_ref in enumerate((y1_ref, y2_ref, y3_ref), start=1):
        y = y + p[:, k:k + 1] * y_ref[...]
    o_ref[0] = x_ref[0] + gate_ref[0] * _rms(y, g_ref[...])


def _combine(x_new, y_slots, top_p, mod, g_post, *, batch, n_tiles, n_lat_tiles, rows):
    tiles_per_batch = rows // TOKEN_TILE
    tiles_per_plane = batch * tiles_per_batch

    def plane(k):
        return pl.BlockSpec((TOKEN_TILE, D_MODEL),
                            lambda b, t: (k * tiles_per_plane + b * tiles_per_batch + t, 0))
    return pl.pallas_call(
        _combine_kernel,
        grid=(batch, n_tiles),
        in_specs=[
            pl.BlockSpec((1, TOKEN_TILE, D_MODEL), lambda b, t: (b, t, 0)),
            plane(0), plane(1), plane(2), plane(3),
            pl.BlockSpec((1, TOKEN_TILE, LANES), lambda b, t: (b, t, 0)),
            _mod_spec(5, n_lat_tiles, batch),
            pl.BlockSpec((1, D_MODEL), lambda b, t: (0, 0)),
        ],
        out_specs=pl.BlockSpec((1, TOKEN_TILE, D_MODEL), lambda b, t: (b, t, 0)),
        out_shape=jax.ShapeDtypeStruct((batch, rows, D_MODEL), F32),
        compiler_params=_params(("arbitrary", "arbitrary"), VMEM_LIMIT),
        name="combine",
    )(x_new, y_slots, y_slots, y_slots, y_slots, top_p, mod, g_post.reshape(1, D_MODEL))


def _rope_tables(n_lat, n_ctx):
    pos = jnp.arange(n_lat, dtype=jnp.int32)
    row = (pos // GRID_W).astype(F32)
    col = (pos % GRID_W).astype(F32)
    n_freq = HEAD_DIM // 4
    inv_freq = ROPE_THETA ** (-jnp.arange(n_freq, dtype=F32) / n_freq)
    ang_r = row[:, None] * inv_freq[None, :]
    ang_c = col[:, None] * inv_freq[None, :]
    ang = jnp.concatenate([ang_r, ang_r, ang_c, ang_c], axis=-1)
    sign = jnp.where((jnp.arange(HEAD_DIM) % 64) < 32, -1.0, 1.0).astype(F32)
    cos = jnp.concatenate([jnp.cos(ang), jnp.ones((n_ctx, HEAD_DIM), F32)], axis=0)
    sin = jnp.concatenate([jnp.sin(ang) * sign[None, :], jnp.zeros((n_ctx, HEAD_DIM), F32)], axis=0)
    return cos, sin


def kernel(x, c, ctx, c_ctx, w_ada, b_ada, g_pre_mix, g_post_mix, g_pre_ffn, g_post_ffn, w_in, q_norm_g, k_norm_g, w_attn_o, conv_dw_w, conv_dw_b, conv_ln_g, conv_ln_b, w_conv_o, w_fnet_o, w_out, router_w, router_b, w_gate, b_gate, w_up, b_up, w_down, b_down):
    batch, n_lat, _ = x.shape
    n_ctx = ctx.shape[1]
    depth = w_ada.shape[0]
    assert n_ctx == TOKEN_TILE and n_lat % TOKEN_TILE == 0 and n_lat % GRID_W == 0
    assert batch + 1 <= MOD_ROWS and x.shape[2] == D_MODEL
    rows = n_lat + n_ctx
    n_lat_tiles = n_lat // TOKEN_TILE
    ctx_tile = n_lat_tiles

    c_rows = jnp.concatenate(
        [c, c_ctx[None, :], jnp.zeros((MOD_ROWS - batch - 1, D_MODEL), F32)], axis=0)
    mod_all = _adaln(c_rows, w_ada, b_ada).reshape(depth, MOD_ROWS * N_MOD, 1, D_MODEL)
    cos, sin = _rope_tables(n_lat, n_ctx)
    cs128 = _channel_dft()
    w_dft_lat = _dft_tables(n_lat)
    w_dft_ctx = _dft_tables(n_ctx)
    pad_lanes = LANES - N_EXPERTS

    x_all = jnp.concatenate([x, ctx], axis=1)
    for l in range(depth):
        last = l == depth - 1
        with_ctx = not last
        n_tiles = n_lat_tiles + (1 if with_ctx else 0)
        rows_l = n_tiles * TOKEN_TILE
        mod = mod_all[l]
        w_in_bf = w_in[l].astype(BF16)

        u_all = _inproj(x_all, mod, g_pre_mix[l], w_in_bf, batch=batch, first_tile=0,
                        n_tiles=n_tiles, n_lat_tiles=n_lat_tiles, out_rows=rows_l)
        if with_ctx:
            kv_ctx, kc_blk = u_all, (ctx_tile, OFF_K // HEAD_DIM, OFF_V // HEAD_DIM)
        else:
            kv_ctx = _inproj(x_all, mod, g_pre_mix[l], w_in_bf[:, OFF_K:OFF_CONV], batch=batch,
                             first_tile=ctx_tile, n_tiles=1, n_lat_tiles=n_lat_tiles, out_rows=n_ctx)
            kc_blk = (0, 0, KV_DIM // HEAD_DIM)
        attn_o = _attention(u_all, kv_ctx, kc_blk, cos, sin, q_norm_g[l], k_norm_g[l], batch=batch,
                            n_lat=n_lat, n_ctx=n_ctx, n_q_tiles=n_tiles, out_rows=rows_l)
        conv_h = _conv_branch(u_all, conv_dw_w[l], conv_dw_b[l], conv_ln_g[l], conv_ln_b[l],
                              batch=batch, n_lat=n_lat, n_ctx=n_ctx, with_ctx=with_ctx)
        fnet_f = _fnet_branch(u_all, cs128, w_dft_lat, w_dft_ctx, batch=batch, n_lat=n_lat,
                              n_ctx=n_ctx, with_ctx=with_ctx)
        rw = jnp.concatenate([router_w[l], jnp.zeros((D_MODEL, pad_lanes), F32)], axis=1)
        rb = jnp.concatenate([router_b[l], jnp.full((pad_lanes,), NEG_BIG, F32)]).reshape(1, LANES)
        x_new, h_ffn, top_i, top_p = _mixer_tail(
            x_all, attn_o, conv_h, fnet_f, u_all, mod, g_post_mix[l], g_pre_ffn[l],
            w_attn_o[l].astype(BF16), w_conv_o[l].astype(BF16), w_fnet_o[l].astype(BF16),
            w_out[l].astype(BF16), rw, rb, batch=batch, n_tiles=n_tiles, n_lat_tiles=n_lat_tiles,
            rows=rows_l)
        routing = _route(top_i, batch=batch, rows=rows_l)
        y_slots = _moe(h_ffn.reshape(batch * rows_l, D_MODEL), routing,
                       w_gate[l], b_gate[l], w_up[l], b_up[l], w_down[l], b_down[l])
        x_all = _combine(x_new, y_slots, top_p, mod, g_post_ffn[l], batch=batch, n_tiles=n_tiles,
                         n_lat_tiles=n_lat_tiles, rows=rows_l)
    return x_all
```
